```python
import jax, jax.numpy as jnp
from jax import lax
import numpy as np

D_MODEL = 1024
BATCH = 1
SEQ = 16384
DEPTH = 1
DEC_BATCH = 128
DEC_SEQ = 1
PAST_LEN = 8192
PAGE_SIZE = 128

HEAD_DIM = 64
N_HEADS = D_MODEL // HEAD_DIM
H_A = N_HEADS // 2
H_B = N_HEADS - H_A
KV_A = H_A // 2
KV_B = H_B // 2
ROT_DIM = HEAD_DIM // 4
ROPE_THETA = 500000.0
MOBA_BLOCK = 256
MOBA_TOPK = 3
MOBA_Q_CHUNK = 64
FOX_Q_CHUNK = 128
N_GROUPS = 4
EXPERTS_PER_GROUP = 4
N_EXPERTS = N_GROUPS * EXPERTS_PER_GROUP
TOPK_IN_GROUP = 2
D_EXPERT = D_MODEL // 2
MOE_TOKEN_CHUNK = 2048
IN_WIDTH = (H_A + 2 * KV_A + H_B + 2 * KV_B) * HEAD_DIM + H_B
FORGET_BIAS_MEAN = 3.0
RMS_EPS = 1e-6
NEG_INF = -1e30
POOL_NUM = 5
POOL_DEN = 4

kernel_name = "hymba_moba_fox_hmoe_decode_step"


def rmsnorm(x, g):
    x32 = x.astype(jnp.float32)
    y = x32 * lax.rsqrt(jnp.mean(x32 * x32, axis=-1, keepdims=True) + RMS_EPS)
    return (y * g.astype(jnp.float32)).astype(x.dtype)


def partial_rope(x, pos):
    half = ROT_DIM // 2
    inv = jnp.power(ROPE_THETA, -2.0 * jnp.arange(half, dtype=jnp.float32) / ROT_DIM)
    ang = pos.astype(jnp.float32)[:, None] * inv[None, :]
    cos = jnp.cos(ang)[None, :, None, :]
    sin = jnp.sin(ang)[None, :, None, :]
    x32 = x.astype(jnp.float32)
    x1, x2 = x32[..., :half], x32[..., half:ROT_DIM]
    out = jnp.concatenate([x1 * cos - x2 * sin, x2 * cos + x1 * sin, x32[..., ROT_DIM:]], axis=-1)
    return out.astype(x.dtype)


def map_chunks(fn, xs, chunk):
    n = xs[0].shape[0]
    c = min(chunk, n)
    nb = -(-n // c)
    pad = nb * c - n
    blocks = tuple(jnp.pad(a, [(0, pad)] + [(0, 0)] * (a.ndim - 1)).reshape((nb, c) + a.shape[1:]) for a in xs)
    out = lax.map(lambda a: fn(*a), blocks)
    return out.reshape((nb * c,) + out.shape[2:])[:n]


def moba_attention(q, k, v, q_pos):
    b, seq_k = k.shape[0], k.shape[1]
    n_blk = -(-seq_k // MOBA_BLOCK)
    pad = n_blk * MOBA_BLOCK - seq_k

    def to_blocks(a):
        a = jnp.pad(a, ((0, 0), (0, pad), (0, 0), (0, 0)))
        return a.reshape(b, n_blk, MOBA_BLOCK, a.shape[2], HEAD_DIM).transpose(0, 3, 1, 2, 4)

    k_blk = to_blocks(k)
    v_blk = to_blocks(v)
    kv_of_head = jnp.arange(H_A) // (H_A // KV_A)
    k_mean = jnp.mean(k_blk.astype(jnp.float32), axis=3)[:, kv_of_head]
    n_top = min(MOBA_TOPK, n_blk)
    scale = HEAD_DIM ** -0.5
    b_idx = jnp.arange(b)[:, None, None, None]
    h_idx = kv_of_head[None, None, :, None]
    offs = jnp.arange(MOBA_BLOCK)

    def chunk(q_c, pos_c):
        qc = q_c.swapaxes(0, 1)
        cur = pos_c // MOBA_BLOCK
        gate = jnp.einsum('bchd,bhnd->bchn', qc.astype(jnp.float32), k_mean)
        full_past = jnp.arange(n_blk)[None, :] < cur[:, None]
        gate = jnp.where(full_past[None, :, None, :], gate, NEG_INF)
        _, top = lax.top_k(gate, n_top)
        top_ok = top < cur[None, :, None, None]
        own = jnp.broadcast_to(cur[None, :, None, None], top.shape[:3] + (1,))
        sel = jnp.concatenate([top, own], axis=-1)
        kg = k_blk[b_idx, h_idx, sel]
        vg = v_blk[b_idx, h_idx, sel]
        own_ok = (cur[:, None] * MOBA_BLOCK + offs[None, :]) <= pos_c[:, None]
        ok = jnp.concatenate([
            jnp.broadcast_to(top_ok[..., None], top.shape + (MOBA_BLOCK,)),
            jnp.broadcast_to(own_ok[None, :, None, None, :], top.shape[:3] + (1, MOBA_BLOCK))], axis=3)
        logits = jnp.einsum('bchd,bchskd->bchsk', qc, kg).astype(jnp.float32) * scale
        logits = jnp.where(ok, logits, NEG_INF)
        shp = logits.shape
        p = jax.nn.softmax(logits.reshape(shp[:3] + (-1,)), axis=-1).reshape(shp)
        out = jnp.einsum('bchsk,bchskd->bchd', p.astype(v.dtype), vg)
        return out.swapaxes(0, 1)

    out = map_chunks(chunk, [q.swapaxes(0, 1), q_pos], MOBA_Q_CHUNK)
    return out.swapaxes(0, 1)


def fox_attention(q, k, v, cum_q, cum_k, q_pos):
    b, seq_k = k.shape[0], k.shape[1]
    rep = H_B // KV_B
    scale = HEAD_DIM ** -0.5
    ck = cum_k.reshape(b, seq_k, KV_B, rep).transpose(0, 2, 3, 1)[:, :, :, None, :]
    k_pos = jnp.arange(seq_k)

    def chunk(q_c, cq_c, pos_c):
        c = q_c.shape[0]
        qc = q_c.swapaxes(0, 1).reshape(b, c, KV_B, rep, HEAD_DIM)
        cq = cq_c.swapaxes(0, 1).reshape(b, c, KV_B, rep).transpose(0, 2, 3, 1)[..., None]
        s = jnp.einsum('bcgrd,blgd->bgrcl', qc, k).astype(jnp.float32) * scale + (cq - ck)
        s = jnp.where(k_pos[None, :] <= pos_c[:, None], s, NEG_INF)
        p = jax.nn.softmax(s, axis=-1)
        o = jnp.einsum('bgrcl,blgd->bcgrd', p.astype(v.dtype), v)
        return o.reshape(b, c, H_B, HEAD_DIM).swapaxes(0, 1)

    out = map_chunks(chunk, [q.swapaxes(0, 1), cum_q.swapaxes(0, 1), q_pos], FOX_Q_CHUNK)
    return out.swapaxes(0, 1)


def mixer_projections(h, pos, w_in, b_f, qn_a, kn_a, qn_b, kn_b):
    b, t, _ = h.shape
    widths = [H_A * HEAD_DIM, KV_A * HEAD_DIM, KV_A * HEAD_DIM, H_B * HEAD_DIM, KV_B * HEAD_DIM, KV_B * HEAD_DIM, H_B]
    cuts = []
    acc = 0
    for w in widths[:-1]:
        acc += w
        cuts.append(acc)
    z = h @ w_in
    qa, ka, va, qb, kb, vb, fg = jnp.split(z, cuts, axis=-1)
    qa = partial_rope(rmsnorm(qa.reshape(b, t, H_A, HEAD_DIM), qn_a), pos)
    ka = partial_rope(rmsnorm(ka.reshape(b, t, KV_A, HEAD_DIM), kn_a), pos)
    va = va.reshape(b, t, KV_A, HEAD_DIM)
    qb = rmsnorm(qb.reshape(b, t, H_B, HEAD_DIM), qn_b)
    kb = rmsnorm(kb.reshape(b, t, KV_B, HEAD_DIM), kn_b)
    vb = vb.reshape(b, t, KV_B, HEAD_DIM)
    logf = jax.nn.log_sigmoid((fg + b_f).astype(jnp.float32))
    return qa, ka, va, qb, kb, vb, logf


def hier_moe(h, w_r1, b_r1, w_r2, b_r2, w_gate, w_up, w_down):
    def chunk(hc):
        n = hc.shape[0]
        p1 = jax.nn.softmax((hc @ w_r1 + b_r1).astype(jnp.float32), axis=-1)
        pg, grp = lax.top_k(p1, 1)
        l2 = (hc @ w_r2 + b_r2).astype(jnp.float32).reshape(n, N_GROUPS, EXPERTS_PER_GROUP)
        l2g = jnp.take_along_axis(l2, grp[:, :, None], axis=1)[:, 0]
        w2, e2 = lax.top_k(jax.nn.softmax(l2g, axis=-1), TOPK_IN_GROUP)
        w2 = w2 / jnp.sum(w2, axis=-1, keepdims=True)
        eid = grp * EXPERTS_PER_GROUP + e2
        comb = jnp.sum(jax.nn.one_hot(eid, N_EXPERTS, dtype=jnp.float32) * (pg * w2)[..., None], axis=1)
        hg = jnp.einsum('nd,edf->nef', hc, w_gate)
        hu = jnp.einsum('nd,edf->nef', hc, w_up)
        a = jax.nn.silu(hg) * hu * comb[:, :, None].astype(hc.dtype)
        return jnp.einsum('nef,efd->nd', a, w_down)
    return map_chunks(chunk, [h], MOE_TOKEN_CHUNK)


def layer_forward(x, c, past, norm1_g, norm2_g, w_ada, b_ada, w_in, b_f, qn_a, kn_a, qn_b, kn_b,
                  w_o, w_r1, b_r1, w_r2, b_r2, w_gate, w_up, w_down):
    b, t, d = x.shape
    n_past = 0 if past is None else past[0].shape[1]
    pos = n_past + jnp.arange(t, dtype=jnp.int32)
    mod = (jax.nn.silu(c) @ w_ada + b_ada)[:, None, :]
    sh1, sc1, g1, sh2, sc2, g2 = jnp.split(mod, 6, axis=-1)
    h = rmsnorm(x, norm1_g) * (1 + sc1) + sh1
    qa, ka, va, qb, kb, vb, logf = mixer_projections(h, pos, w_in, b_f, qn_a, kn_a, qn_b, kn_b)
    if past is None:
        ka_all, va_all, kb_all, vb_all, lf_all = ka, va, kb, vb, logf
    else:
        ka_all = jnp.concatenate([past[0], ka], axis=1)
        va_all = jnp.concatenate([past[1], va], axis=1)
        kb_all = jnp.concatenate([past[2], kb], axis=1)
        vb_all = jnp.concatenate([past[3], vb], axis=1)
        lf_all = jnp.concatenate([past[4].astype(jnp.float32), logf], axis=1)
    cum = jnp.cumsum(lf_all, axis=1)
    oa = moba_attention(qa, ka_all, va_all, pos)
    ob = fox_attention(qb, kb_all, vb_all, cum[:, n_past:], cum, pos)
    o = jnp.concatenate([oa.reshape(b, t, -1), ob.reshape(b, t, -1)], axis=-1) @ w_o
    x = x + g1 * o
    h2 = rmsnorm(x, norm2_g) * (1 + sc2) + sh2
    x = x + g2 * hier_moe(h2.reshape(b * t, d), w_r1, b_r1, w_r2, b_r2, w_gate, w_up, w_down).reshape(b, t, d)
    return x, (ka, va, kb, vb, logf.astype(x.dtype))


def setup_inputs(seed: int = 0) -> dict:
    key = jax.random.key(seed)
    ks = jax.random.split(key, 32)
    f32 = jnp.float32
    n_pages = PAST_LEN // PAGE_SIZE
    n_used = DEC_BATCH * n_pages
    n_pool = n_used * POOL_NUM // POOL_DEN
    nrm = lambda k, shp, s: jax.random.normal(k, shp, f32) * s
    page_table = jax.random.permutation(ks[0], n_pool)[:n_used].reshape(DEC_BATCH, n_pages).astype(jnp.int32)
    d_inv = D_MODEL ** -0.5
    return {
        "x_prompt": nrm(ks[1], (BATCH, SEQ, D_MODEL), 1.0),
        "x_sample": nrm(ks[2], (DEC_BATCH, DEC_SEQ, D_MODEL), 1.0),
        "cache_moba_k": nrm(ks[3], (DEPTH, n_pool, PAGE_SIZE, KV_A, HEAD_DIM), 1.0),
        "cache_moba_v": nrm(ks[4], (DEPTH, n_pool, PAGE_SIZE, KV_A, HEAD_DIM), 1.0),
        "cache_fox_k": nrm(ks[5], (DEPTH, n_pool, PAGE_SIZE, KV_B, HEAD_DIM), 1.0),
        "cache_fox_v": nrm(ks[6], (DEPTH, n_pool, PAGE_SIZE, KV_B, HEAD_DIM), 1.0),
        "cache_fox_logf": jax.nn.log_sigmoid(FORGET_BIAS_MEAN + nrm(ks[7], (DEPTH, n_pool, PAGE_SIZE, H_B), 1.0)),
        "page_table": page_table,
        "c_prompt": nrm(ks[8], (BATCH, D_MODEL), 1.0),
        "c_sample": nrm(ks[9], (DEC_BATCH, D_MODEL), 1.0),
        "norm1_g": 1.0 + nrm(ks[10], (DEPTH, D_MODEL), 0.02),
        "norm2_g": 1.0 + nrm(ks[11], (DEPTH, D_MODEL), 0.02),
        "w_ada": nrm(ks[12], (DEPTH, D_MODEL, 6 * D_MODEL), 0.5 * d_inv),
        "b_ada": nrm(ks[13], (DEPTH, 6 * D_MODEL), 0.02),
        "w_in": nrm(ks[14], (DEPTH, D_MODEL, IN_WIDTH), d_inv),
        "b_f": FORGET_BIAS_MEAN + nrm(ks[15], (DEPTH, H_B), 0.1),
        "qn_a": 1.0 + nrm(ks[16], (DEPTH, HEAD_DIM), 0.02),
        "kn_a": 1.0 + nrm(ks[17], (DEPTH, HEAD_DIM), 0.02),
        "qn_b": 1.0 + nrm(ks[18], (DEPTH, HEAD_DIM), 0.02),
        "kn_b": 1.0 + nrm(ks[19], (DEPTH, HEAD_DIM), 0.02),
        "w_o": nrm(ks[20], (DEPTH, D_MODEL, D_MODEL), d_inv),
        "w_r1": nrm(ks[21], (DEPTH, D_MODEL, N_GROUPS), d_inv),
        "b_r1": nrm(ks[22], (DEPTH, N_GROUPS), 0.01),
        "w_r2": nrm(ks[23], (DEPTH, D_MODEL, N_EXPERTS), d_inv),
        "b_r2": nrm(ks[24], (DEPTH, N_EXPERTS), 0.01),
        "w_gate": nrm(ks[25], (DEPTH, N_EXPERTS, D_MODEL, D_EXPERT), d_inv),
        "w_up": nrm(ks[26], (DEPTH, N_EXPERTS, D_MODEL, D_EXPERT), d_inv),
        "w_down": nrm(ks[27], (DEPTH, N_EXPERTS, D_EXPERT, D_MODEL), D_EXPERT ** -0.5),
    }


def reference(x_prompt, x_sample, cache_moba_k, cache_moba_v, cache_fox_k, cache_fox_v, cache_fox_logf,
              page_table, c_prompt, c_sample, norm1_g, norm2_g, w_ada, b_ada, w_in, b_f,
              qn_a, kn_a, qn_b, kn_b, w_o, w_r1, b_r1, w_r2, b_r2, w_gate, w_up, w_down):
    n_dec, n_pages = page_table.shape
    past_len = n_pages * cache_moba_k.shape[2]

    def paged(cache, layer):
        rows = cache[layer, page_table]
        return rows.reshape((n_dec, past_len) + cache.shape[3:])

    y_prompt, y_sample = x_prompt, x_sample
    new_p = [[], [], [], [], []]
    new_s = [[], [], [], [], []]
    for layer in range(DEPTH):
        weights = (norm1_g[layer], norm2_g[layer], w_ada[layer], b_ada[layer], w_in[layer], b_f[layer],
                   qn_a[layer], kn_a[layer], qn_b[layer], kn_b[layer], w_o[layer], w_r1[layer], b_r1[layer],
                   w_r2[layer], b_r2[layer], w_gate[layer], w_up[layer], w_down[layer])
        y_prompt, st_p = layer_forward(y_prompt, c_prompt, None, *weights)
        past = (paged(cache_moba_k, layer), paged(cache_moba_v, layer), paged(cache_fox_k, layer),
                paged(cache_fox_v, layer), paged(cache_fox_logf, layer))
        y_sample, st_s = layer_forward(y_sample, c_sample, past, *weights)
        for lst, a in zip(new_p, st_p):
            lst.append(a)
        for lst, a in zip(new_s, st_s):
            lst.append(a)
    moba_k_prompt = jnp.stack(new_p[0])
    moba_v_prompt = jnp.stack(new_p[1])
    fox_k_prompt = jnp.stack(new_p[2])
    fox_v_prompt = jnp.stack(new_p[3])
    fox_logf_prompt = jnp.stack(new_p[4])
    moba_k_sample = jnp.stack(new_s[0])
    moba_v_sample = jnp.stack(new_s[1])
    fox_k_sample = jnp.stack(new_s[2])
    fox_v_sample = jnp.stack(new_s[3])
    fox_logf_sample = jnp.stack(new_s[4])
    return (y_prompt, y_sample, moba_k_prompt, moba_v_prompt, fox_k_prompt, fox_v_prompt, fox_logf_prompt,
            moba_k_sample, moba_v_sample, fox_k_sample, fox_v_sample, fox_logf_sample)
```

```python
import functools

import numpy as np
import jax
import jax.numpy as jnp
from jax import lax
from jax.experimental import pallas as pl
from jax.experimental.pallas import tpu as pltpu

F32 = jnp.float32
BF16 = jnp.bfloat16
HIGHEST = lax.Precision.HIGHEST

LANES = 128
HEAD_DIM = 64
H_A, KV_A, H_B, KV_B = 8, 4, 8, 4
ROT_DIM = 16
ROPE_THETA = 500000.0
MOBA_BLOCK = 256
MOBA_TOPK = 3
N_GROUPS = 4
EXPERTS_PER_GROUP = 4
N_EXPERTS = 16
RMS_EPS = 1e-6
NEG_INF = -1e30
LOWEST = -3e38
PAGES_PER_STEP = 8
VMEM_LIMIT = 56 * 1024 * 1024

QA_W, KA_W, QB_W, KB_W = H_A * HEAD_DIM, KV_A * HEAD_DIM, H_B * HEAD_DIM, KV_B * HEAD_DIM
MAIN_W = QA_W + 2 * KA_W + QB_W + 2 * KB_W
FG_OFF = 64


def _cparams(*sem):
    return pltpu.CompilerParams(dimension_semantics=sem, vmem_limit_bytes=VMEM_LIMIT)


def _nt_dot(a, b, precision=None):
    return lax.dot_general(a, b, (((1,), (1,)), ((), ())), precision=precision,
                           preferred_element_type=F32)


def _mm(a, b):
    return jnp.dot(a, b, precision=HIGHEST if b.dtype == F32 else None, preferred_element_type=F32)


def _ada_kernel(c_ref, w_ref, b_ref, o_ref):
    c = c_ref[...]
    s = c * jax.nn.sigmoid(c)
    o_ref[...] = jnp.dot(s, w_ref[...], precision=HIGHEST, preferred_element_type=F32) + b_ref[...]


def _ada(c_all, w_ada, b_ada):
    m, d = c_all.shape
    n = w_ada.shape[1]
    tn = 1536
    return pl.pallas_call(
        _ada_kernel,
        grid=(n // tn,),
        in_specs=[pl.BlockSpec((m, d), lambda j: (0, 0)),
                  pl.BlockSpec((d, tn), lambda j: (0, j)),
                  pl.BlockSpec((1, tn), lambda j: (0, j))],
        out_specs=pl.BlockSpec((m, tn), lambda j: (0, j)),
        out_shape=jax.ShapeDtypeStruct((m, n), F32),
        compiler_params=_cparams("parallel"),
        name="ada",
    )(c_all, w_ada, b_ada.reshape(1, n))


def _rope_kernel(inv_ref, c_ref, s1_ref, s2_ref, *, pos0, tm, same_pos):
    shape = (tm, LANES)
    lane = lax.broadcasted_iota(jnp.int32, shape, 1) & (HEAD_DIM - 1)
    if same_pos:
        pos = jnp.full(shape, pos0, jnp.int32)
    else:
        pos = pos0 + pl.program_id(0) * tm + lax.broadcasted_iota(jnp.int32, shape, 0)
    ang = pos.astype(F32) * inv_ref[...]
    cs = jnp.cos(ang)
    sn = jnp.sin(ang)
    half = ROT_DIM // 2
    c_ref[...] = jnp.where(lane < ROT_DIM, cs, 1.0)
    s1_ref[...] = jnp.where(lane < half, -sn, 0.0)
    s2_ref[...] = jnp.where((lane >= half) & (lane < ROT_DIM), sn, 0.0)


def _rope_tables(n_tok, pos0, same_pos, tm):
    half = ROT_DIM // 2
    inv = jnp.power(ROPE_THETA, -2.0 * jnp.arange(half, dtype=F32) / ROT_DIM)
    lane = np.arange(LANES) % HEAD_DIM
    inv_row = jnp.where(jnp.asarray(lane < ROT_DIM), inv[lane % half], 0.0).reshape(1, LANES)
    out = jax.ShapeDtypeStruct((n_tok, LANES), F32)
    spec = pl.BlockSpec((tm, LANES), lambda i: (i, 0))
    return pl.pallas_call(
        functools.partial(_rope_kernel, pos0=pos0, tm=tm, same_pos=same_pos),
        grid=(n_tok // tm,),
        in_specs=[pl.BlockSpec((1, LANES), lambda i: (0, 0))],
        out_specs=[spec, spec, spec],
        out_shape=[out, out, out],
        compiler_params=_cparams("parallel"),
        name="rope_tables",
    )(inv_row)


def _inproj_kernel(x_ref, sh_ref, sc_ref, g_ref, w_ref, bf_ref, gain_ref, bd_ref, c_ref, s1_ref, s2_ref,
                   qa_ref, ka_ref, va_ref, qb_ref, kb_ref, vb_ref, lf_ref, km_ref, *, tm, with_kmean):
    x = x_ref[...]
    h = x * lax.rsqrt(jnp.mean(x * x, axis=-1, keepdims=True) + RMS_EPS) * g_ref[...]
    h = h * (1.0 + sc_ref[...]) + sh_ref[...]
    hb = h.astype(w_ref.dtype)
    bd = bd_ref[...]
    cs, s1, s2 = c_ref[...], s1_ref[...], s2_ref[...]

    def project(off, width):
        return _mm(hb, w_ref[:, off:off + width])

    def head_norm(z, gain_row, out_ref, rope):
        for c in range(z.shape[1] // LANES):
            zc = z[:, c * LANES:(c + 1) * LANES]
            msq = _mm((zc * zc).astype(bd.dtype), bd)
            y = zc * lax.rsqrt(msq + RMS_EPS) * gain_ref[gain_row:gain_row + 1, :]
            if rope:
                y = y * cs + pltpu.roll(y, LANES - ROT_DIM // 2, 1) * s1 + pltpu.roll(y, ROT_DIM // 2, 1) * s2
            out_ref[:, c * LANES:(c + 1) * LANES] = y

    off = 0
    head_norm(project(off, QA_W), 0, qa_ref, True)
    off += QA_W
    head_norm(project(off, KA_W), 1, ka_ref, True)
    off += KA_W
    va_ref[...] = project(off, KA_W)
    off += KA_W
    head_norm(project(off, QB_W), 2, qb_ref, False)
    off += QB_W
    head_norm(project(off, KB_W), 3, kb_ref, False)
    off += KB_W
    vb_ref[...] = project(off, KB_W)
    off += KB_W
    z = project(off, LANES) + bf_ref[...]
    lf = jnp.minimum(z, 0.0) - jnp.log(1.0 + jnp.exp(-jnp.abs(z)))
    lane = lax.broadcasted_iota(jnp.int32, (tm, LANES), 1)
    lf_ref[...] = jnp.where(lane < H_B, lf, 0.0)
    if with_kmean:
        for j in range(tm // MOBA_BLOCK):
            km_ref[j] = jnp.mean(ka_ref[j * MOBA_BLOCK:(j + 1) * MOBA_BLOCK, :], axis=0, keepdims=True)
    else:
        km_ref[...] = jnp.zeros(km_ref.shape, F32)


def _inproj(x, sh, sc, norm_g, w_cat, bf_row, gains, bd, tables, tm, per_token_mod):
    n_tok, d = x.shape
    with_kmean = tm % MOBA_BLOCK == 0
    n_km = max(tm // MOBA_BLOCK, 1)
    mod_spec = (pl.BlockSpec((tm, d), lambda i: (i, 0)) if per_token_mod
                else pl.BlockSpec((1, d), lambda i: (0, 0)))
    const = lambda shape: pl.BlockSpec(shape, lambda i: (0,) * len(shape))
    row = lambda w: pl.BlockSpec((tm, w), lambda i: (i, 0))
    out_shape = [jax.ShapeDtypeStruct((n_tok, w), F32) for w in (QA_W, KA_W, KA_W, QB_W, KB_W, KB_W, LANES)]
    out_shape.append(jax.ShapeDtypeStruct((n_tok // tm * n_km, 1, KA_W), F32))
    return pl.pallas_call(
        functools.partial(_inproj_kernel, tm=tm, with_kmean=with_kmean),
        grid=(n_tok // tm,),
        in_specs=[row(d), mod_spec, mod_spec, const((1, d)), const(w_cat.shape), const((1, LANES)),
                  const(gains.shape), const(bd.shape), row(LANES), row(LANES), row(LANES)],
        out_specs=[row(QA_W), row(KA_W), row(KA_W), row(QB_W), row(KB_W), row(KB_W), row(LANES),
                   pl.BlockSpec((n_km, 1, KA_W), lambda i: (i, 0, 0))],
        out_shape=out_shape,
        compiler_params=_cparams("parallel"),
        name="inproj",
    )(x, sh, sc, norm_g, w_cat, bf_row, gains, bd, *tables)


def _cumsum_kernel(lf_ref, tri_ref, cum_ref, carry_ref):
    @pl.when(pl.program_id(0) == 0)
    def _():
        carry_ref[...] = jnp.zeros(carry_ref.shape, F32)

    c = jnp.dot(tri_ref[...], lf_ref[...], precision=HIGHEST, preferred_element_type=F32) + carry_ref[...]
    cum_ref[...] = c
    carry_ref[...] = c[c.shape[0] - 1:, :]


def _cumsum(lf, tm):
    n_tok = lf.shape[0]
    tri = jnp.asarray(np.tril(np.ones((tm, tm), np.float32)))
    return pl.pallas_call(
        _cumsum_kernel,
        grid=(n_tok // tm,),
        in_specs=[pl.BlockSpec((tm, LANES), lambda i: (i, 0)), pl.BlockSpec((tm, tm), lambda i: (0, 0))],
        out_specs=pl.BlockSpec((tm, LANES), lambda i: (i, 0)),
        out_shape=jax.ShapeDtypeStruct((n_tok, LANES), F32),
        scratch_shapes=[pltpu.VMEM((1, LANES), F32)],
        compiler_params=_cparams("arbitrary"),
        name="logf_cumsum",
    )(lf, tri)


def _pair_member(chunk, r):
    return chunk if r == 0 else pltpu.roll(chunk, HEAD_DIM, 1)


def _moba_aug_kernel(qa_ref, ka_ref, va_ref, km_ref, qaug_ref, kaug_ref, vh_ref, *, tq):
    shape = (tq, LANES)
    lane = lax.broadcasted_iota(jnp.int32, shape, 1)
    pos = pl.program_id(0) * tq + lax.broadcasted_iota(jnp.int32, shape, 0)
    cur = pos // MOBA_BLOCK
    blk = lane - FG_OFF
    upper = lane >= FG_OFF
    past = upper & (blk < cur)
    past_f = jnp.where(past, 1.0, 0.0)
    q = qa_ref[...]
    gate = jnp.dot(q, km_ref[...], precision=HIGHEST, preferred_element_type=F32)
    for c in range(H_A // 2):
        gch = gate[:, c * LANES:(c + 1) * LANES]
        qch = q[:, c * LANES:(c + 1) * LANES] * (HEAD_DIM ** -0.5)
        for r in range(2):
            g = _pair_member(gch, 1 - r)
            vals = jnp.where(upper, jnp.where(past, g, NEG_INF), LOWEST)
            sel = jnp.zeros(shape, F32)
            for _ in range(MOBA_TOPK):
                m = jnp.max(vals, axis=1, keepdims=True)
                idx = jnp.min(jnp.where(vals == m, lane, 2 * LANES), axis=1, keepdims=True)
                pick = lane == idx
                sel = jnp.where(pick, past_f, sel)
                vals = jnp.where(pick, LOWEST, vals)
            pen = jnp.where(blk == cur, 0.0, jnp.where(sel > 0.0, 0.0, NEG_INF))
            qaug_ref[2 * c + r] = jnp.where(upper, pen, _pair_member(qch, r)).astype(BF16)
    onehot = jnp.where(blk == cur, 1.0, 0.0)
    for c in range(KV_A // 2):
        kch = ka_ref[:, c * LANES:(c + 1) * LANES]
        vch = va_ref[:, c * LANES:(c + 1) * LANES]
        for r in range(2):
            kaug_ref[2 * c + r] = jnp.where(upper, onehot, _pair_member(kch, r)).astype(BF16)
            vh_ref[2 * c + r] = _pair_member(vch, r)[:, :HEAD_DIM].astype(BF16)


def _fox_aug_kernel(qb_ref, kb_ref, vb_ref, cum_ref, pq_ref, pk_ref, oq_ref, ok_ref,
                    qaug_ref, kaug_ref, vh_ref, *, tq):
    shape = (tq, LANES)
    upper = lax.broadcasted_iota(jnp.int32, shape, 1) >= FG_OFF
    cum = cum_ref[...]
    hi = cum.astype(BF16)
    rem = cum - hi.astype(F32)
    mid = rem.astype(BF16)
    lo = (rem - mid.astype(F32)).astype(BF16)
    parts = jnp.concatenate([hi, mid, lo], axis=1)
    for c in range(H_B // 2):
        qch = qb_ref[:, c * LANES:(c + 1) * LANES] * (HEAD_DIM ** -0.5)
        for r in range(2):
            h = 2 * c + r
            extra = jnp.dot(parts, pq_ref[h], preferred_element_type=F32) + oq_ref[h]
            qaug_ref[h] = jnp.where(upper, extra, _pair_member(qch, r)).astype(BF16)
    for c in range(KV_B // 2):
        kch = kb_ref[:, c * LANES:(c + 1) * LANES]
        vch = vb_ref[:, c * LANES:(c + 1) * LANES]
        for r in range(2):
            g = 2 * c + r
            extra = jnp.dot(parts, pk_ref[g], preferred_element_type=F32) + ok_ref[...]
            kaug_ref[g] = jnp.where(upper, extra, _pair_member(kch, r)).astype(BF16)
            vh_ref[g] = _pair_member(vch, r)[:, :HEAD_DIM].astype(BF16)


def _fox_placement():
    pq = np.zeros((H_B, 3 * LANES, LANES), np.float32)
    pk = np.zeros((KV_B, 3 * LANES, LANES), np.float32)
    oq = np.zeros((H_B, 1, LANES), np.float32)
    ok = np.zeros((1, LANES), np.float32)
    ok[0, FG_OFF:FG_OFF + 3] = 1.0
    for h in range(H_B):
        r = h % 2
        for j in range(3):
            pq[h, j * LANES + h, FG_OFF + j] = 1.0
            pk[h // 2, j * LANES + h, FG_OFF + 3 + 3 * r + j] = -1.0
        oq[h, 0, FG_OFF + 3 + 3 * r:FG_OFF + 6 + 3 * r] = 1.0
    return jnp.asarray(pq, BF16), jnp.asarray(pk, BF16), jnp.asarray(oq), jnp.asarray(ok)


def _aug_specs(n_tok, tq):
    row = lambda w: pl.BlockSpec((tq, w), lambda i: (i, 0))
    heads = lambda n, w: pl.BlockSpec((n, tq, w), lambda i: (0, i, 0))
    out_specs = [heads(H_A, LANES), heads(KV_A, LANES), heads(KV_A, HEAD_DIM)]
    out_shape = [jax.ShapeDtypeStruct((H_A, n_tok, LANES), BF16),
                 jax.ShapeDtypeStruct((KV_A, n_tok, LANES), BF16),
                 jax.ShapeDtypeStruct((KV_A, n_tok, HEAD_DIM), BF16)]
    return row, out_specs, out_shape


def _moba_aug(qa, ka, va, km_all, tq):
    n_tok = qa.shape[0]
    row, out_specs, out_shape = _aug_specs(n_tok, tq)
    return pl.pallas_call(
        functools.partial(_moba_aug_kernel, tq=tq),
        grid=(n_tok // tq,),
        in_specs=[row(QA_W), row(KA_W), row(KA_W), pl.BlockSpec(km_all.shape, lambda i: (0, 0))],
        out_specs=out_specs, out_shape=out_shape,
        compiler_params=_cparams("parallel"),
        name="moba_aug",
    )(qa, ka, va, km_all)


def _fox_aug(qb, kb, vb, cum, tq):
    n_tok = qb.shape[0]
    pq, pk, oq, ok = _fox_placement()
    row, out_specs, out_shape = _aug_specs(n_tok, tq)
    const = lambda a: pl.BlockSpec(a.shape, lambda i: (0,) * a.ndim)
    return pl.pallas_call(
        functools.partial(_fox_aug_kernel, tq=tq),
        grid=(n_tok // tq,),
        in_specs=[row(QB_W), row(KB_W), row(KB_W), row(LANES), const(pq), const(pk), const(oq), const(ok)],
        out_specs=out_specs, out_shape=out_shape,
        compiler_params=_cparams("parallel"),
        name="fox_aug",
    )(qb, kb, vb, cum, pq, pk, oq, ok)


def _flash_kernel(q_ref, k_ref, v_ref, o_ref, m_ref, l_ref, acc_ref, *, t):
    qi = pl.program_id(1)
    ki = pl.program_id(2)

    @pl.when(ki == 0)
    def _():
        m_ref[...] = jnp.full(m_ref.shape, LOWEST, F32)
        l_ref[...] = jnp.zeros(l_ref.shape, F32)
        acc_ref[...] = jnp.zeros(acc_ref.shape, F32)

    def step(diagonal):
        k = k_ref[0]
        v = v_ref[0]
        for r in range(2):
            s = _nt_dot(q_ref[r], k)
            if diagonal:
                row = lax.broadcasted_iota(jnp.int32, (t, t), 0)
                col = lax.broadcasted_iota(jnp.int32, (t, t), 1)
                s = jnp.where(col <= row, s, NEG_INF)
            m_prev = m_ref[r]
            m_new = jnp.maximum(m_prev, jnp.max(s, axis=1, keepdims=True))
            alpha = jnp.exp(m_prev - m_new)
            p = jnp.exp(s - m_new)
            l_ref[r] = alpha * l_ref[r] + jnp.sum(p, axis=1, keepdims=True)
            acc_ref[r] = alpha * acc_ref[r] + jnp.dot(p.astype(BF16), v, preferred_element_type=F32)
            m_ref[r] = m_new

    @pl.when(ki < qi)
    def _():
        step(False)

    @pl.when(ki == qi)
    def _():
        step(True)
        o_ref[...] = jnp.concatenate([acc_ref[0] / l_ref[0], acc_ref[1] / l_ref[1]], axis=1).astype(o_ref.dtype)


def _flash(q_aug, k_aug, v_h, t):
    n_kv, n_tok, _ = k_aug.shape
    n_t = n_tok // t
    return pl.pallas_call(
        functools.partial(_flash_kernel, t=t),
        grid=(n_kv, n_t, n_t),
        in_specs=[pl.BlockSpec((2, t, LANES), lambda g, qi, ki: (g, qi, 0)),
                  pl.BlockSpec((1, t, LANES), lambda g, qi, ki: (g, jnp.minimum(ki, qi), 0)),
                  pl.BlockSpec((1, t, HEAD_DIM), lambda g, qi, ki: (g, jnp.minimum(ki, qi), 0))],
        out_specs=pl.BlockSpec((t, LANES), lambda g, qi, ki: (qi, g)),
        out_shape=jax.ShapeDtypeStruct((n_tok, 2 * n_kv * HEAD_DIM), BF16),
        scratch_shapes=[pltpu.VMEM((2, t, 1), F32), pltpu.VMEM((2, t, 1), F32), pltpu.VMEM((2, t, HEAD_DIM), F32)],
        compiler_params=_cparams("parallel", "parallel", "arbitrary"),
        name="flash",
    )(q_aug, k_aug, v_h)


def _lfprefix_kernel(l_ref, t_ref, o_ref):
    l = l_ref[...]
    hi = l.astype(BF16)
    rem = l - hi.astype(F32)
    mid = rem.astype(BF16)
    lo = (rem - mid.astype(F32)).astype(BF16)
    t = t_ref[...]
    dot = lambda a: jnp.dot(a, t, preferred_element_type=F32)
    o_ref[...] = dot(hi) + dot(mid) + dot(lo)


def _lfprefix(lf_rows):
    n_rows, page = lf_rows.shape
    t = np.arange(page)[:, None] <= np.arange(page)[None, :]
    tm = _tile(n_rows, 2048)
    return pl.pallas_call(
        _lfprefix_kernel,
        grid=(n_rows // tm,),
        in_specs=[pl.BlockSpec((tm, page), lambda i: (i, 0)), pl.BlockSpec((page, page), lambda i: (0, 0))],
        out_specs=pl.BlockSpec((tm, page), lambda i: (i, 0)),
        out_shape=jax.ShapeDtypeStruct((n_rows, page), F32),
        compiler_params=_cparams("parallel"),
        name="logf_page_prefix",
    )(lf_rows, jnp.asarray(t, BF16))


def _decode_kernel(pt_ref, *refs, n_pages, page):
    del pt_ref
    n = PAGES_PER_STEP
    ka, va, kb, vb, cl = (refs[i * n:(i + 1) * n] for i in range(5))
    (qa_ref, qaf_ref, qb_ref, kna_ref, vna_ref, knb_ref, vnb_ref, lfn_ref, pe_ref,
     oa_ref, ob_ref, ma_ref, la_ref, oa_s, ks_s, mb_ref, lb_ref, accb_ref, off_ref) = refs[5 * n:]
    c = pl.program_id(1)
    lane = lax.broadcasted_iota(jnp.int32, (H_A, LANES), 1)
    ks_lane = lax.broadcasted_iota(jnp.int32, ks_s.shape, 1)
    per_blk = MOBA_BLOCK // page

    @pl.when(c == 0)
    def _():
        ma_ref[...] = jnp.full(ma_ref.shape, LOWEST, F32)
        la_ref[...] = jnp.zeros(la_ref.shape, F32)
        ks_s[...] = jnp.zeros(ks_s.shape, F32)
        mb_ref[...] = jnp.full(mb_ref.shape, LOWEST, F32)
        lb_ref[...] = jnp.zeros(lb_ref.shape, F32)
        accb_ref[...] = jnp.zeros(accb_ref.shape, F32)
        off_ref[...] = jnp.zeros(off_ref.shape, F32)

    qa = qa_ref[0]
    qb = qb_ref[0]
    m_all, l_all = ma_ref[...], la_ref[...]
    m_b, l_b, acc_b, off = mb_ref[...], lb_ref[...], accb_ref[...], off_ref[...]
    ks = ks_s[...]
    for j in range(n):
        p = c * n + j
        k_t = ka[j][0]
        s = jnp.dot(qa, k_t.astype(BF16), preferred_element_type=F32)
        m_p = jnp.max(s, axis=1, keepdims=True)
        e = jnp.exp(s - m_p)
        m_all = jnp.where(lane == p, m_p, m_all)
        l_all = jnp.where(lane == p, jnp.sum(e, axis=1, keepdims=True), l_all)
        oa_s[p] = _nt_dot(e.astype(BF16), va[j][0].astype(BF16))
        ks = ks + jnp.where(ks_lane == p // per_blk, jnp.sum(k_t, axis=1, keepdims=True), 0.0)
        cum = cl[j][0]
        s = jnp.dot(qb, kb[j][0].astype(BF16), preferred_element_type=F32) - (off + cum)
        off = off + cum[:, page - 1:page]
        m_new = jnp.maximum(m_b, jnp.max(s, axis=1, keepdims=True))
        alpha = jnp.exp(m_b - m_new)
        e = jnp.exp(s - m_new)
        l_b = alpha * l_b + jnp.sum(e, axis=1, keepdims=True)
        acc_b = alpha * acc_b + _nt_dot(e.astype(BF16), vb[j][0].astype(BF16))
        m_b = m_new
    ks_s[...] = ks
    ma_ref[...], la_ref[...] = m_all, l_all
    mb_ref[...], lb_ref[...], accb_ref[...], off_ref[...] = m_b, l_b, acc_b, off

    @pl.when(c == n_pages // n - 1)
    def _():
        n_blk = n_pages * page // MOBA_BLOCK
        row = lax.broadcasted_iota(jnp.int32, (H_A, HEAD_DIM), 0)

        def own_head(o):
            out = jnp.zeros((H_A, HEAD_DIM), F32)
            for g in range(KV_A):
                out = out + jnp.where(row // (H_A // KV_A) == g, o[:, g * HEAD_DIM:(g + 1) * HEAD_DIM], 0.0)
            return out

        gate = jnp.dot(qaf_ref[0], ks, precision=HIGHEST, preferred_element_type=F32) * (1.0 / MOBA_BLOCK)
        vals = jnp.where(lane < n_blk, gate, LOWEST)
        sel = jnp.zeros((H_A, LANES), F32)
        for _ in range(min(MOBA_TOPK, n_blk)):
            m = jnp.max(vals, axis=1, keepdims=True)
            idx = jnp.min(jnp.where(vals == m, lane, 2 * LANES), axis=1, keepdims=True)
            pick = lane == idx
            sel = jnp.where(pick, 1.0, sel)
            vals = jnp.where(pick, LOWEST, vals)
        sel_pages = jnp.dot(sel.astype(BF16), pe_ref[...], preferred_element_type=F32) > 0.5
        s_self = jnp.sum(qa.astype(F32) * kna_ref[0], axis=1, keepdims=True)
        m_tot = jnp.maximum(jnp.max(jnp.where(sel_pages, m_all, LOWEST), axis=1, keepdims=True), s_self)
        w = jnp.where(sel_pages, jnp.exp(m_all - m_tot), 0.0)
        e_self = jnp.exp(s_self - m_tot)
        l_tot = jnp.sum(w * l_all, axis=1, keepdims=True) + e_self
        o = e_self * vna_ref[0]
        for pg in range(n_pages):
            o = o + w[:, pg:pg + 1] * oa_s[pg]
        oa_ref[0] = own_head(o / l_tot)

        s_self = jnp.sum(qb.astype(F32) * knb_ref[0], axis=1, keepdims=True) - (off + lfn_ref[0])
        m_tot = jnp.maximum(m_b, s_self)
        a_past = jnp.exp(m_b - m_tot)
        e_self = jnp.exp(s_self - m_tot)
        o = (a_past * acc_b + e_self * vnb_ref[0]) / (a_past * l_b + e_self)
        ob_ref[0] = own_head(o)


def _decode(page_table, caches, cum_pages, qa_bd, qa_bd_f32, qb_bd, kna, vna, knb, vnb, lfn):
    n_dec, n_pages = page_table.shape
    ck_a, cv_a, ck_b, cv_b = caches
    width, page = ck_a.shape[1], ck_a.shape[2]
    n = PAGES_PER_STEP
    assert n_pages % n == 0 and (n_pages * page) % MOBA_BLOCK == 0 and MOBA_BLOCK % page == 0
    assert n_pages <= LANES and page == LANES
    per_blk = MOBA_BLOCK // page
    idx = np.arange(LANES)
    pe = (idx[:, None] == idx[None, :] // per_blk) & (idx[None, :] < n_pages)

    def paged(shape, j):
        return pl.BlockSpec(shape, lambda b, c, pt: (pt[b * n_pages + c * n + j], 0, 0))

    per_dec = lambda a: pl.BlockSpec((1,) + a.shape[1:], lambda b, c, pt: (b, 0, 0))
    const = lambda a: pl.BlockSpec(a.shape, lambda b, c, pt: (0, 0))
    in_specs, operands = [], []
    for arr in (ck_a, cv_a, ck_b, cv_b, cum_pages):
        for j in range(n):
            in_specs.append(paged((1,) + arr.shape[1:], j))
            operands.append(arr)
    pe_a = jnp.asarray(pe, BF16)
    for arr in (qa_bd, qa_bd_f32, qb_bd, kna, vna, knb, vnb, lfn):
        in_specs.append(per_dec(arr))
        operands.append(arr)
    in_specs.append(const(pe_a))
    operands.append(pe_a)
    out_spec = pl.BlockSpec((1, H_A, HEAD_DIM), lambda b, c, pt: (b, 0, 0))
    out = jax.ShapeDtypeStruct((n_dec, H_A, HEAD_DIM), F32)
    grid_spec = pltpu.PrefetchScalarGridSpec(
        num_scalar_prefetch=1,
        grid=(n_dec, n_pages // n),
        in_specs=in_specs,
        out_specs=[out_spec, out_spec],
        scratch_shapes=[pltpu.VMEM((H_A, LANES), F32), pltpu.VMEM((H_A, LANES), F32),
                        pltpu.VMEM((n_pages, H_A, width), F32), pltpu.VMEM((width, LANES), F32),
                        pltpu.VMEM((H_B, 1), F32), pltpu.VMEM((H_B, 1), F32),
                        pltpu.VMEM((H_B, width), F32), pltpu.VMEM((H_B, 1), F32)])
    return pl.pallas_call(
        functools.partial(_decode_kernel, n_pages=n_pages, page=page),
        grid_spec=grid_spec,
        out_shape=[out, out],
        compiler_params=_cparams("arbitrary", "arbitrary"),
        name="decode_attention",
    )(page_table.reshape(-1), *operands)


def _post_kernel(oa_ref, ob_ref, x_ref, g1_ref, sh_ref, sc_ref, n2_ref, wo_ref, wr_ref, br_ref,
                 x1_ref, h2_ref, comb_ref, *, tm):
    half = wo_ref.shape[0] // 2
    o = _mm(oa_ref[...], wo_ref[:half, :]) + _mm(ob_ref[...], wo_ref[half:, :])
    x1 = x_ref[...] + g1_ref[...] * o
    x1_ref[...] = x1
    h2 = x1 * lax.rsqrt(jnp.mean(x1 * x1, axis=-1, keepdims=True) + RMS_EPS) * n2_ref[...]
    h2 = h2 * (1.0 + sc_ref[...]) + sh_ref[...]
    h2_ref[...] = h2.astype(BF16)
    logits = jnp.dot(h2, wr_ref[...], precision=HIGHEST, preferred_element_type=F32) + br_ref[...]
    lane = lax.broadcasted_iota(jnp.int32, (tm, LANES), 1)
    big = 2 * LANES
    l1 = jnp.where(lane < N_GROUPS, logits, LOWEST)
    m1 = jnp.max(l1, axis=1, keepdims=True)
    pg = 1.0 / jnp.sum(jnp.exp(l1 - m1), axis=1, keepdims=True)
    grp = jnp.min(jnp.where(l1 == m1, lane, big), axis=1, keepdims=True)
    e_lane = lane - N_GROUPS
    in_grp = (e_lane >= 0) & (e_lane < N_EXPERTS) & (e_lane // EXPERTS_PER_GROUP == grp)
    l2 = jnp.where(in_grp, logits, LOWEST)
    m2 = jnp.max(l2, axis=1, keepdims=True)
    e2 = jnp.exp(l2 - m2)
    sm = jnp.where(in_grp, e2 / jnp.sum(e2, axis=1, keepdims=True), -1.0)
    v1 = jnp.max(sm, axis=1, keepdims=True)
    i1 = jnp.min(jnp.where(sm == v1, lane, big), axis=1, keepdims=True)
    sm2 = jnp.where(lane == i1, -2.0, sm)
    v2 = jnp.max(sm2, axis=1, keepdims=True)
    i2 = jnp.min(jnp.where(sm2 == v2, lane, big), axis=1, keepdims=True)
    tot = v1 + v2
    comb_ref[...] = pg * jnp.where(lane == i1, v1 / tot, jnp.where(lane == i2, v2 / tot, 0.0))


def _post(oa, ob, x, g1, sh2, sc2, norm_g, w_o, w_r, b_r, tm, per_token_mod):
    n_tok, d = x.shape
    mod_spec = (pl.BlockSpec((tm, d), lambda i: (i, 0)) if per_token_mod
                else pl.BlockSpec((1, d), lambda i: (0, 0)))
    const = lambda a: pl.BlockSpec(a.shape, lambda i: (0,) * a.ndim)
    row = lambda w: pl.BlockSpec((tm, w), lambda i: (i, 0))
    return pl.pallas_call(
        functools.partial(_post_kernel, tm=tm),
        grid=(n_tok // tm,),
        in_specs=[row(oa.shape[1]), row(ob.shape[1]), row(d), mod_spec, mod_spec, mod_spec,
                  const(norm_g), const(w_o), const(w_r), const(b_r)],
        out_specs=[row(d), row(d), row(LANES)],
        out_shape=[jax.ShapeDtypeStruct((n_tok, d), F32), jax.ShapeDtypeStruct((n_tok, d), BF16),
                   jax.ShapeDtypeStruct((n_tok, LANES), F32)],
        compiler_params=_cparams("parallel"),
        name="outproj_router",
    )(oa, ob, x, g1, sh2, sc2, norm_g, w_o, w_r, b_r)


def _moe_kernel(h2_ref, comb_ref, x1_ref, g2_ref, wg_ref, wu_ref, wd_ref, y_ref, acc_ref, *, tm):
    e = pl.program_id(1)

    @pl.when(e == 0)
    def _():
        acc_ref[...] = jnp.zeros(acc_ref.shape, F32)

    h = h2_ref[...]
    hg = jnp.dot(h, wg_ref[0], preferred_element_type=F32)
    hu = jnp.dot(h, wu_ref[0], preferred_element_type=F32)
    lane = lax.broadcasted_iota(jnp.int32, (tm, LANES), 1)
    cw = jnp.sum(jnp.where(lane == e + N_GROUPS, comb_ref[...], 0.0), axis=1, keepdims=True)
    a = hg * jax.nn.sigmoid(hg) * hu * cw
    acc_ref[...] += jnp.dot(a.astype(BF16), wd_ref[0], preferred_element_type=F32)

    @pl.when(e == pl.num_programs(1) - 1)
    def _():
        y_ref[...] = x1_ref[...] + g2_ref[...] * acc_ref[...]


def _moe(h2, comb, x1, g2, w_gate, w_up, w_down, tm, per_token_mod):
    n_tok, d = x1.shape
    n_e, _, f = w_gate.shape
    mod_spec = (pl.BlockSpec((tm, d), lambda i, e: (i, 0)) if per_token_mod
                else pl.BlockSpec((1, d), lambda i, e: (0, 0)))
    row = lambda w: pl.BlockSpec((tm, w), lambda i, e: (i, 0))
    return pl.pallas_call(
        functools.partial(_moe_kernel, tm=tm),
        grid=(n_tok // tm, n_e),
        in_specs=[row(d), row(LANES), row(d), mod_spec,
                  pl.BlockSpec((1, d, f), lambda i, e: (e, 0, 0)),
                  pl.BlockSpec((1, d, f), lambda i, e: (e, 0, 0)),
                  pl.BlockSpec((1, f, d), lambda i, e: (e, 0, 0))],
        out_specs=row(d),
        out_shape=jax.ShapeDtypeStruct((n_tok, d), F32),
        scratch_shapes=[pltpu.VMEM((tm, d), F32)],
        compiler_params=_cparams("parallel", "arbitrary"),
        name="moe_experts",
    )(h2, comb, x1, g2, w_gate, w_up, w_down)


def _tile(n, pref):
    t = pref
    while n % t:
        t //= 2
    return t


def _kmean_blockdiag(km, n_tok):
    n_blk = n_tok // MOBA_BLOCK
    assert n_blk <= HEAD_DIM
    km = km.reshape(-1, KV_A, HEAD_DIM)[:n_blk]
    km = jnp.pad(km, ((0, HEAD_DIM - n_blk), (0, 0), (0, 0)))
    per_head = jnp.transpose(km[:, np.arange(H_A) // (H_A // KV_A), :], (1, 2, 0))
    eye = jnp.asarray(np.eye(H_A, dtype=np.float32))
    return jnp.einsum("hdb,hg->hdgb", per_head, eye).reshape(QA_W, QA_W)


def _block_diag_rows(q, n_kv):
    b = q.shape[0]
    n_h = q.shape[1] // HEAD_DIM
    onehot = jnp.asarray((np.arange(n_h)[:, None] // (n_h // n_kv) == np.arange(n_kv)[None, :]).astype(np.float32))
    return jnp.einsum("bhd,hg->bhgd", q.reshape(b, n_h, HEAD_DIM), onehot).reshape(b, n_h, n_kv * HEAD_DIM)


def kernel(x_prompt, x_sample, cache_moba_k, cache_moba_v, cache_fox_k, cache_fox_v, cache_fox_logf, page_table,
           c_prompt, c_sample, norm1_g, norm2_g, w_ada, b_ada, w_in, b_f, qn_a, kn_a, qn_b, kn_b, w_o,
           w_r1, b_r1, w_r2, b_r2, w_gate, w_up, w_down):
    depth = norm1_g.shape[0]
    assert depth == 1 and x_prompt.shape[0] == 1 and x_sample.shape[1] == 1
    _, seq, d = x_prompt.shape
    n_dec, n_pages = page_table.shape
    n_pool, page = cache_moba_k.shape[1], cache_moba_k.shape[2]
    past_len = n_pages * page
    assert seq % MOBA_BLOCK == 0

    w_cat_f32 = jnp.concatenate([w_in[0], jnp.zeros((d, LANES - H_B), F32)], axis=1)
    w_cat = w_cat_f32.astype(BF16)
    bf_row = jnp.pad(b_f[0], (0, LANES - H_B)).reshape(1, LANES)
    gains = jnp.stack([jnp.tile(g[0], 2) for g in (qn_a, kn_a, qn_b, kn_b)])
    bd = jnp.asarray(np.kron(np.eye(2), np.full((HEAD_DIM, HEAD_DIM), 1.0 / HEAD_DIM)), BF16)
    w_r = jnp.pad(jnp.concatenate([w_r1[0], w_r2[0]], axis=1), ((0, 0), (0, LANES - N_GROUPS - N_EXPERTS)))
    b_r = jnp.pad(jnp.concatenate([b_r1[0], b_r2[0]]), (0, LANES - N_GROUPS - N_EXPERTS)).reshape(1, LANES)
    w_o_b, w_g_b, w_u_b, w_d_b = (w[0].astype(BF16) for w in (w_o, w_gate, w_up, w_down))
    n1, n2 = norm1_g[0].reshape(1, d), norm2_g[0].reshape(1, d)

    pad = (-(1 + n_dec)) % 8
    c_all = jnp.concatenate([c_prompt, c_sample, jnp.zeros((pad, d), F32)], axis=0)
    mod = _ada(c_all, w_ada[0], b_ada[0])
    mod_p = [mod[0:1, i * d:(i + 1) * d] for i in range(6)]
    mod_s = [mod[1:1 + n_dec, i * d:(i + 1) * d] for i in range(6)]

    xp = x_prompt[0]
    tm = _tile(seq, 512)
    tables = _rope_tables(seq, 0, False, tm)
    qa, ka, va, qb, kb, vb, lf, km = _inproj(xp, mod_p[0], mod_p[1], n1, w_cat, bf_row, gains, bd, tables, tm, False)
    cum = _cumsum(lf, tm)
    qa_aug, ka_aug, va_h = _moba_aug(qa, ka, va, _kmean_blockdiag(km, seq), tm)
    qb_aug, kb_aug, vb_h = _fox_aug(qb, kb, vb, cum, tm)
    oa = _flash(qa_aug, ka_aug, va_h, tm)
    ob = _flash(qb_aug, kb_aug, vb_h, tm)
    x1, h2, comb = _post(oa, ob, xp, mod_p[2], mod_p[3], mod_p[4], n2, w_o_b, w_r, b_r, tm, False)
    y_prompt = _moe(h2, comb, x1, mod_p[5], w_g_b, w_u_b, w_d_b, tm, False)

    xs = x_sample[:, 0]
    ts = _tile(n_dec, 128)
    tables_s = _rope_tables(n_dec, past_len, True, ts)
    qa_s, ka_s, va_s, qb_s, kb_s, vb_s, lf_s, _ = _inproj(xs, mod_s[0], mod_s[1], n1, w_cat_f32, bf_row, gains,
                                                         bd.astype(F32), tables_s, ts, True)
    lf_t = jnp.transpose(cache_fox_logf[0], (0, 2, 1)).reshape(n_pool * H_B, page)
    cum_pages = _lfprefix(lf_t).reshape(n_pool, H_B, page)
    caches = [jnp.transpose(c[0], (0, 2, 3, 1)).reshape(n_pool, -1, page)
              for c in (cache_moba_k, cache_moba_v, cache_fox_k, cache_fox_v)]
    scale = HEAD_DIM ** -0.5
    qa_bd = _block_diag_rows(qa_s, KV_A)
    qb_bd = _block_diag_rows(qb_s, KV_B)
    row3 = lambda a: a.reshape(n_dec, 1, -1)
    oa_s, ob_s = _decode(page_table, caches, cum_pages, (qa_bd * scale).astype(BF16), qa_bd,
                         (qb_bd * scale).astype(BF16), row3(ka_s), row3(va_s), row3(kb_s), row3(vb_s),
                         lf_s[:, :H_B].reshape(n_dec, H_B, 1))
    x1_s, h2_s, comb_s = _post(oa_s.reshape(n_dec, -1), ob_s.reshape(n_dec, -1), xs,
                               mod_s[2], mod_s[3], mod_s[4], n2, w_o[0], w_r, b_r, ts, True)
    y_sample = _moe(h2_s, comb_s, x1_s, mod_s[5], w_g_b, w_u_b, w_d_b, ts, True)

    heads = lambda a, b, n: a.reshape(1, b, -1, n, HEAD_DIM)
    return (y_prompt[None], y_sample[:, None],
            heads(ka, 1, KV_A), heads(va, 1, KV_A), heads(kb, 1, KV_B), heads(vb, 1, KV_B),
            lf[:, :H_B].reshape(1, 1, seq, H_B),
            heads(ka_s, n_dec, KV_A), heads(va_s, n_dec, KV_A), heads(kb_s, n_dec, KV_B), heads(vb_s, n_dec, KV_B),
            lf_s[:, :H_B].reshape(1, n_dec, 1, H_B))
```
